```python
import jax
import jax.numpy as jnp
from jax import lax
import numpy as np

D_MODEL = 2048
BATCH = 1
SEQ = 8192
DEPTH = 1

SSM_HEADS = 32
SSM_HEAD_DIM = 64
SSM_GROUPS = 8
SSM_HPG = SSM_HEADS // SSM_GROUPS
SSM_STATE = 128
SSM_INNER = SSM_HEADS * SSM_HEAD_DIM
SSM_CHUNK = 128
DN_K_HEADS = 8
DN_V_HEADS = 16
DN_HEAD_K = 128
DN_HEAD_V = 128
DN_KEY_DIM = DN_K_HEADS * DN_HEAD_K
DN_VAL_DIM = DN_V_HEADS * DN_HEAD_V
DN_CHUNK = 64
CONV_WIDTH = 5
N_DIR = 2
N_BRANCH = 2
CONV_DIM = SSM_INNER + 2 * SSM_GROUPS * SSM_STATE + 2 * DN_KEY_DIM + DN_VAL_DIM
PROJ_SIZES = (CONV_DIM, SSM_INNER, DN_VAL_DIM, N_DIR * SSM_HEADS, N_DIR * DN_V_HEADS, N_DIR * DN_V_HEADS, N_BRANCH * D_MODEL)
IN_DIM = sum(PROJ_SIZES)
CONV_SIZES = (SSM_INNER, SSM_GROUPS * SSM_STATE, SSM_GROUPS * SSM_STATE, DN_KEY_DIM, DN_KEY_DIM, DN_VAL_DIM)
DEEPNORM_ALPHA = (2 * DEPTH) ** 0.25
DEEPNORM_BETA = (8 * DEPTH) ** -0.25
RMS_EPS = 1e-6
LN_EPS = 1e-5

kernel_name = "bidir_ssd_deltanet_gated_merge_deepnorm"


def _split(t, sizes):
    idx = np.cumsum(np.array(sizes[:-1])).tolist()
    return jnp.split(t, idx, axis=-1)


def _flip(t):
    return jnp.flip(t, axis=1)


def _rmsnorm(x, w):
    xf = x.astype(jnp.float32)
    return xf * lax.rsqrt(jnp.mean(xf * xf, axis=-1, keepdims=True) + RMS_EPS) * w.astype(jnp.float32)


def _layernorm(x, g, b):
    xf = x.astype(jnp.float32)
    mu = jnp.mean(xf, axis=-1, keepdims=True)
    xc = xf - mu
    var = jnp.mean(xc * xc, axis=-1, keepdims=True)
    return xc * lax.rsqrt(var + LN_EPS) * g.astype(jnp.float32) + b.astype(jnp.float32)


def _l2norm(x):
    return x * lax.rsqrt(jnp.sum(x * x, axis=-1, keepdims=True) + 1e-6)


def _centred_dwconv(u, w, b):
    ch = u.shape[-1]
    pad = (CONV_WIDTH - 1) // 2
    y = lax.conv_general_dilated(u, w[:, None, :].astype(u.dtype), (1,), [(pad, pad)],
                                 dimension_numbers=('NWC', 'WIO', 'NWC'), feature_group_count=ch)
    return y + b.astype(u.dtype)


def _ssd_chunked(x, dt, a, bm, cm):
    bsz, s, g, hg, p = x.shape
    q = SSM_CHUNK
    n = s // q
    x = x.reshape(bsz, n, q, g, hg, p)
    dt = dt.reshape(bsz, n, q, g, hg)
    bm = bm.reshape(bsz, n, q, g, -1)
    cm = cm.reshape(bsz, n, q, g, -1)
    a_cum = jnp.cumsum(dt * a, axis=2)
    xdt = x * dt[..., None]
    tril = jnp.tril(jnp.ones((q, q), dtype=bool))
    seg = a_cum[:, :, :, None] - a_cum[:, :, None, :]
    lmat = jnp.exp(jnp.where(tril[:, :, None, None], seg, -jnp.inf))
    cb = jnp.einsum('bnlgk,bnsgk->bnlsg', cm, bm)
    y_diag = jnp.einsum('bnlsgh,bnsghp->bnlghp', cb[..., None] * lmat, xdt)
    decay_states = jnp.exp(a_cum[:, :, -1:] - a_cum)
    states = jnp.einsum('bnsgk,bnsghp->bnghpk', bm, xdt * decay_states[..., None])
    chunk_decay = jnp.exp(a_cum[:, :, -1])

    def step(h, inp):
        st, dec = inp
        return h * dec[..., None, None] + st, h

    h0 = jnp.zeros_like(states[:, 0])
    _, prev = lax.scan(step, h0, (jnp.moveaxis(states, 1, 0), jnp.moveaxis(chunk_decay, 1, 0)))
    prev = jnp.moveaxis(prev, 0, 1)
    y_off = jnp.einsum('bnlgk,bnghpk->bnlghp', cm, prev) * jnp.exp(a_cum)[..., None]
    return (y_diag + y_off).reshape(bsz, s, g, hg, p)


def _gated_delta_chunked(q, k, v, beta, g):
    bsz, s, h, dk = q.shape
    dv = v.shape[-1]
    c = DN_CHUNK
    n = s // c

    def chunks(t):
        return jnp.moveaxis(t.reshape((bsz, n, c, h) + t.shape[3:]), 3, 1)

    q, k, v, beta, g = chunks(q), chunks(k), chunks(v), chunks(beta), chunks(g)
    g_cum = jnp.cumsum(g, axis=-1)
    tril = jnp.tril(jnp.ones((c, c), dtype=bool))
    strict = jnp.tril(jnp.ones((c, c), dtype=bool), -1)
    decay = jnp.exp(jnp.where(tril, g_cum[..., :, None] - g_cum[..., None, :], -jnp.inf))
    k_beta = k * beta[..., None]
    v_beta = v * beta[..., None]
    a_mat = jnp.where(strict, jnp.einsum('bhnid,bhnjd->bhnij', k_beta, k) * decay, 0.0)
    eye = jnp.eye(c, dtype=a_mat.dtype)
    t_inv = lax.linalg.triangular_solve(eye + a_mat, jnp.broadcast_to(eye, a_mat.shape),
                                        left_side=True, lower=True)
    u = t_inv @ v_beta
    w = t_inv @ (k_beta * jnp.exp(g_cum)[..., None])
    attn = jnp.where(tril, jnp.einsum('bhnid,bhnjd->bhnij', q, k) * decay, 0.0)
    q_dec = q * jnp.exp(g_cum)[..., None]
    k_dec = k * jnp.exp(g_cum[..., -1:] - g_cum)[..., None]
    last = jnp.exp(g_cum[..., -1])

    def step(state, inp):
        u_i, w_i, q_i, k_i, attn_i, last_i = inp
        v_new = u_i - w_i @ state
        o = q_i @ state + attn_i @ v_new
        state = state * last_i[..., None, None] + jnp.einsum('bhck,bhcv->bhkv', k_i, v_new)
        return state, o

    xs = tuple(jnp.moveaxis(t, 2, 0) for t in (u, w, q_dec, k_dec, attn, last))
    state0 = jnp.zeros((bsz, h, dk, dv), dtype=u.dtype)
    _, o = lax.scan(step, state0, xs)
    o = jnp.moveaxis(o, 0, 2)
    return jnp.moveaxis(o, 1, 3).reshape(bsz, s, h, dv)


def setup_inputs(seed: int = 0) -> dict:
    key = jax.random.key(seed)
    ks = jax.random.split(key, 24)
    L, D = DEPTH, D_MODEL
    f32 = jnp.float32

    def nrm(k, shape, std):
        return jax.random.normal(k, shape, f32) * std

    def inv_softplus_dt(k, shape):
        dt = jnp.exp(jax.random.uniform(k, shape, f32, minval=np.log(1e-3), maxval=np.log(1e-1)))
        return dt + jnp.log(-jnp.expm1(-dt))

    return {
        'x': nrm(ks[0], (BATCH, SEQ, D), 1.0),
        'c': nrm(ks[1], (BATCH, D), 1.0),
        'w_ada': nrm(ks[2], (L, D, 3 * D), 0.1 * D ** -0.5),
        'b_ada': nrm(ks[3], (L, 3 * D), 0.02),
        'w_in': nrm(ks[4], (L, D, IN_DIM), D ** -0.5),
        'conv_w': nrm(ks[5], (L, CONV_WIDTH, CONV_DIM), CONV_WIDTH ** -0.5),
        'conv_b': nrm(ks[6], (L, CONV_DIM), 0.02),
        'ssm_a_log': jnp.log(jax.random.uniform(ks[7], (L, N_DIR, SSM_HEADS), f32, minval=1.0, maxval=16.0)),
        'ssm_dt_bias': inv_softplus_dt(ks[8], (L, N_DIR, SSM_HEADS)),
        'ssm_d': 1.0 + nrm(ks[9], (L, SSM_HEADS), 0.1),
        'ssm_norm_w': 1.0 + nrm(ks[10], (L, SSM_INNER), 0.02),
        'dn_a_log': jnp.log(jax.random.uniform(ks[11], (L, N_DIR, DN_V_HEADS), f32, minval=1.0, maxval=16.0)),
        'dn_dt_bias': inv_softplus_dt(ks[12], (L, N_DIR, DN_V_HEADS)),
        'dn_norm_w': 1.0 + nrm(ks[13], (L, DN_HEAD_V), 0.02),
        'w_branch_ssm': nrm(ks[14], (L, SSM_INNER, D), SSM_INNER ** -0.5 * DEEPNORM_BETA),
        'w_branch_dn': nrm(ks[15], (L, DN_VAL_DIM, D), DN_VAL_DIM ** -0.5 * DEEPNORM_BETA),
        'w_out': nrm(ks[16], (L, D, D), D ** -0.5 * DEEPNORM_BETA),
        'ln_g': 1.0 + nrm(ks[17], (L, D), 0.02),
        'ln_b': nrm(ks[18], (L, D), 0.02),
    }


def reference(x, c, w_ada, b_ada, w_in, conv_w, conv_b, ssm_a_log, ssm_dt_bias, ssm_d, ssm_norm_w,
              dn_a_log, dn_dt_bias, dn_norm_w, w_branch_ssm, w_branch_dn, w_out, ln_g, ln_b):
    f32 = jnp.float32
    bsz, s, _ = x.shape
    for l in range(DEPTH):
        shift, scale, gate = jnp.split(c @ w_ada[l] + b_ada[l], 3, axis=-1)
        h = x * (1.0 + scale[:, None]) + shift[:, None]
        proj = h @ w_in[l]
        conv_in, z_ssm, z_dn, dt_raw, a_raw, b_raw, gate_raw = _split(proj, PROJ_SIZES)
        u = jax.nn.silu(_centred_dwconv(conv_in, conv_w[l], conv_b[l])).astype(f32)
        xs_, bm, cm, qd, kd, vd = _split(u, CONV_SIZES)

        xs_ = xs_.reshape(bsz, s, SSM_GROUPS, SSM_HPG, SSM_HEAD_DIM)
        bm = bm.reshape(bsz, s, SSM_GROUPS, SSM_STATE)
        cm = cm.reshape(bsz, s, SSM_GROUPS, SSM_STATE)
        dt_all = jax.nn.softplus(dt_raw.astype(f32).reshape(bsz, s, N_DIR, SSM_HEADS) + ssm_dt_bias[l].astype(f32))
        a_all = -jnp.exp(ssm_a_log[l].astype(f32))
        dt_f = dt_all[:, :, 0].reshape(bsz, s, SSM_GROUPS, SSM_HPG)
        dt_b = dt_all[:, :, 1].reshape(bsz, s, SSM_GROUPS, SSM_HPG)
        y_f = _ssd_chunked(xs_, dt_f, a_all[0].reshape(SSM_GROUPS, SSM_HPG), bm, cm)
        y_b = _flip(_ssd_chunked(_flip(xs_), _flip(dt_b), a_all[1].reshape(SSM_GROUPS, SSM_HPG), _flip(bm), _flip(cm)))
        d_skip = ssm_d[l].astype(f32).reshape(SSM_GROUPS, SSM_HPG)[..., None]
        y_ssm = (y_f + y_b + d_skip * xs_).reshape(bsz, s, SSM_INNER)
        y_ssm = _rmsnorm(y_ssm * jax.nn.silu(z_ssm.astype(f32)), ssm_norm_w[l])

        rep = DN_V_HEADS // DN_K_HEADS
        q = _l2norm(qd.reshape(bsz, s, DN_K_HEADS, DN_HEAD_K)) * (DN_HEAD_K ** -0.5)
        k = _l2norm(kd.reshape(bsz, s, DN_K_HEADS, DN_HEAD_K))
        q = jnp.repeat(q, rep, axis=2)
        k = jnp.repeat(k, rep, axis=2)
        v = vd.reshape(bsz, s, DN_V_HEADS, DN_HEAD_V)
        beta = jax.nn.sigmoid(b_raw.astype(f32).reshape(bsz, s, N_DIR, DN_V_HEADS))
        g_log = -jnp.exp(dn_a_log[l].astype(f32)) * jax.nn.softplus(
            a_raw.astype(f32).reshape(bsz, s, N_DIR, DN_V_HEADS) + dn_dt_bias[l].astype(f32))
        o_f = _gated_delta_chunked(q, k, v, beta[:, :, 0], g_log[:, :, 0])
        o_b = _flip(_gated_delta_chunked(_flip(q), _flip(k), _flip(v), _flip(beta[:, :, 1]), _flip(g_log[:, :, 1])))
        z_dn_h = z_dn.astype(f32).reshape(bsz, s, DN_V_HEADS, DN_HEAD_V)
        y_dn = (_rmsnorm(o_f + o_b, dn_norm_w[l]) * jax.nn.silu(z_dn_h)).reshape(bsz, s, DN_VAL_DIM)

        p_ssm = y_ssm.astype(x.dtype) @ w_branch_ssm[l]
        p_dn = y_dn.astype(x.dtype) @ w_branch_dn[l]
        g_ssm, g_dn = jnp.split(jax.nn.sigmoid(gate_raw), N_BRANCH, axis=-1)
        mixed = (g_ssm * p_ssm + g_dn * p_dn) @ w_out[l]

        x = _layernorm(DEEPNORM_ALPHA * x + gate[:, None] * mixed, ln_g[l], ln_b[l]).astype(x.dtype)
    return x
```

```python
import functools

import jax
import jax.numpy as jnp
from jax import lax
from jax.experimental import pallas as pl
from jax.experimental.pallas import tpu as pltpu

F32 = jnp.float32
BF16 = jnp.bfloat16

D_MODEL = 2048
SSM_HEADS = 32
SSM_HEAD_DIM = 64
SSM_GROUPS = 8
SSM_HPG = SSM_HEADS // SSM_GROUPS
SSM_STATE = 128
SSM_INNER = SSM_HEADS * SSM_HEAD_DIM
DN_K_HEADS = 8
DN_V_HEADS = 16
DN_HEAD = 128
DN_KEY_DIM = DN_K_HEADS * DN_HEAD
DN_VAL_DIM = DN_V_HEADS * DN_HEAD
CONV_WIDTH = 5
CONV_DIM = SSM_INNER + 2 * SSM_GROUPS * SSM_STATE + 2 * DN_KEY_DIM + DN_VAL_DIM
N_SMALL = 2 * SSM_HEADS + 4 * DN_V_HEADS
DEEPNORM_ALPHA = 2.0 ** 0.25
RMS_EPS = 1e-6
LN_EPS = 1e-5

LANES = 128
SSD_CHUNK = 128
DN_CHUNK = 64
NEUMANN_LEVELS = 6
HALO = 16
NEG_BIG = -1e30
VMEM_LIMIT = 56 * 1024 * 1024

SM_DT = 0
SM_A = SSM_HEADS
SM_B = SSM_HEADS + DN_V_HEADS


def _sigmoid(x):
    return 1.0 / (1.0 + jnp.exp(-x))


def _softplus(x):
    return jnp.maximum(x, 0.0) + jnp.log(1.0 + jnp.exp(-jnp.abs(x)))


def _params(**kw):
    return pltpu.CompilerParams(vmem_limit_bytes=VMEM_LIMIT, **kw)


def _adaln_kernel(c_ref, w_ref, b_ref, o_ref):
    o_ref[...] = jnp.dot(c_ref[...], w_ref[...], preferred_element_type=F32,
                         precision=lax.Precision.HIGHEST) + b_ref[...]


def _adaln(c8, w_ada, b_ada):
    n = w_ada.shape[1]
    tn = 768
    return pl.pallas_call(
        _adaln_kernel,
        grid=(n // tn,),
        in_specs=[pl.BlockSpec((8, D_MODEL), lambda j: (0, 0)),
                  pl.BlockSpec((D_MODEL, tn), lambda j: (0, j)),
                  pl.BlockSpec((1, tn), lambda j: (0, j))],
        out_specs=pl.BlockSpec((8, tn), lambda j: (0, j)),
        out_shape=jax.ShapeDtypeStruct((8, n), F32),
        compiler_params=_params(dimension_semantics=("arbitrary",)),
        name="adaln",
    )(c8, w_ada, b_ada)


def _inproj_kernel(x_ref, sc_ref, sh_ref, w_ref, o_ref, h_ref):
    @pl.when(pl.program_id(1) == 0)
    def _():
        h_ref[...] = (x_ref[...] * (1.0 + sc_ref[...]) + sh_ref[...]).astype(BF16)

    o_ref[...] = jnp.dot(h_ref[...], w_ref[...], preferred_element_type=F32).astype(o_ref.dtype)


def _inproj(x2, scale, shift, w, out_dtype, tm, tn, name):
    s = x2.shape[0]
    n = w.shape[1]
    return pl.pallas_call(
        _inproj_kernel,
        grid=(s // tm, n // tn),
        in_specs=[pl.BlockSpec((tm, D_MODEL), lambda i, j: (i, 0)),
                  pl.BlockSpec((1, D_MODEL), lambda i, j: (0, 0)),
                  pl.BlockSpec((1, D_MODEL), lambda i, j: (0, 0)),
                  pl.BlockSpec((D_MODEL, tn), lambda i, j: (0, j))],
        out_specs=pl.BlockSpec((tm, tn), lambda i, j: (i, j)),
        out_shape=jax.ShapeDtypeStruct((s, n), out_dtype),
        scratch_shapes=[pltpu.VMEM((tm, D_MODEL), BF16)],
        compiler_params=_params(dimension_semantics=("arbitrary", "arbitrary")),
        name=name,
    )(x2, scale, shift, w)


CONV_T = 512
CONV_TC = 1024
CONV_RB = 64
Q_BLOCK = (SSM_INNER + 2 * SSM_GROUPS * SSM_STATE) // CONV_TC
K_BLOCK = Q_BLOCK + DN_KEY_DIM // CONV_TC


def _conv_kernel(prev_ref, main_ref, next_ref, w_ref, b_ref, o_ref, ext_ref):
    i = pl.program_id(0)
    j = pl.program_id(1)
    has_prev = (i > 0).astype(F32)
    has_next = (i < pl.num_programs(0) - 1).astype(F32)
    ext_ref[0:HALO, :] = prev_ref[...].astype(F32) * has_prev
    ext_ref[HALO:HALO + CONV_T, :] = main_ref[...].astype(F32)
    ext_ref[HALO + CONV_T:2 * HALO + CONV_T, :] = next_ref[...].astype(F32) * has_next
    w = w_ref[...]
    b = b_ref[...]
    pad = (CONV_WIDTH - 1) // 2
    is_q = j == Q_BLOCK
    is_qk = jnp.logical_or(is_q, j == K_BLOCK)
    qk_scale = jnp.where(is_q, DN_HEAD ** -0.5, 1.0).astype(F32)

    for r in range(CONV_T // CONV_RB):
        r0 = r * CONV_RB
        acc = b
        for t in range(CONV_WIDTH):
            acc = acc + w[t:t + 1, :] * ext_ref[HALO + r0 + t - pad:HALO + r0 + t - pad + CONV_RB, :]
        u = acc * _sigmoid(acc)

        @pl.when(is_qk)
        def _():
            for hh in range(CONV_TC // DN_HEAD):
                uh = u[:, hh * DN_HEAD:(hh + 1) * DN_HEAD]
                inv = lax.rsqrt(jnp.sum(uh * uh, axis=-1, keepdims=True) + 1e-6) * qk_scale
                o_ref[r0:r0 + CONV_RB, hh * DN_HEAD:(hh + 1) * DN_HEAD] = (uh * inv).astype(o_ref.dtype)

        @pl.when(jnp.logical_not(is_qk))
        def _():
            o_ref[r0:r0 + CONV_RB, :] = u.astype(o_ref.dtype)


def _conv(proj, conv_w, conv_b):
    s = proj.shape[0]
    nt = s // CONV_T
    hb = CONV_T // HALO
    last_halo = s // HALO - 1
    return pl.pallas_call(
        _conv_kernel,
        grid=(nt, CONV_DIM // CONV_TC),
        in_specs=[pl.BlockSpec((HALO, CONV_TC), lambda i, j: (jnp.maximum(i * hb - 1, 0), j)),
                  pl.BlockSpec((CONV_T, CONV_TC), lambda i, j: (i, j)),
                  pl.BlockSpec((HALO, CONV_TC), lambda i, j: (jnp.minimum((i + 1) * hb, last_halo), j)),
                  pl.BlockSpec((CONV_WIDTH, CONV_TC), lambda i, j: (0, j)),
                  pl.BlockSpec((1, CONV_TC), lambda i, j: (0, j))],
        out_specs=pl.BlockSpec((CONV_T, CONV_TC), lambda i, j: (i, j)),
        out_shape=jax.ShapeDtypeStruct((s, CONV_DIM), BF16),
        scratch_shapes=[pltpu.VMEM((CONV_T + 2 * HALO, CONV_TC), F32)],
        compiler_params=_params(dimension_semantics=("arbitrary", "arbitrary")),
        name="conv",
    )(proj, proj, proj, conv_w, conv_b)


def _chunk_index(d, c, n):
    return c + d * (n - 1 - 2 * c)


def _tri(d, n):
    row = lax.broadcasted_iota(jnp.int32, (n, n), 0)
    col = lax.broadcasted_iota(jnp.int32, (n, n), 1)
    return (row - col) * (1 - 2 * d) >= 0


def _ssd_kernel(sm_ref, prm_ref, x_ref, b_ref, c_ref, y_ref, h_ref):
    d = pl.program_id(0)
    q = SSD_CHUNK

    @pl.when(pl.program_id(1) == 0)
    def _():
        h_ref[...] = jnp.zeros_like(h_ref)

    lane = lax.broadcasted_iota(jnp.int32, (1, LANES), 1)
    prm = prm_ref[...]
    a_row = jnp.where(lane < SSM_HEADS, -jnp.exp(prm[1:2, :]), 0.0)
    dt = _softplus(sm_ref[...] + prm[0:1, :])
    dta = dt * a_row
    mask = _tri(d, q)
    acum = jnp.dot(mask.astype(F32), dta, preferred_element_type=F32,
                   precision=lax.Precision.HIGHEST)
    total = jnp.sum(dta, axis=0, keepdims=True)
    acum_t = acum.T
    dt_t = dt.T

    erow = lax.broadcasted_iota(jnp.int32, (LANES, SSM_INNER), 0)
    ecol = lax.broadcasted_iota(jnp.int32, (LANES, SSM_INNER), 1)
    expand = (erow == ecol // SSM_HEAD_DIM).astype(BF16)

    def expand_heads(v):
        hi = v.astype(BF16)
        lo = (v - hi.astype(F32)).astype(BF16)
        return (jnp.dot(hi, expand, preferred_element_type=F32)
                + jnp.dot(lo, expand, preferred_element_type=F32))

    w_exp = expand_heads(dt * jnp.exp(total - acum))
    ea_exp = expand_heads(jnp.exp(acum))
    cd_exp = expand_heads(jnp.broadcast_to(jnp.exp(total), (8, LANES)))[0:1, :]

    x = x_ref[...]
    xw = (x.astype(F32) * w_exp).astype(BF16)
    glane = lax.broadcasted_iota(jnp.int32, (1, SSM_HPG * SSM_HEAD_DIM), 1) // SSM_HEAD_DIM

    for g in range(SSM_GROUPS):
        gs = slice(g * SSM_HPG * SSM_HEAD_DIM, (g + 1) * SSM_HPG * SSM_HEAD_DIM)
        ns = slice(g * SSM_STATE, (g + 1) * SSM_STATE)
        bg = b_ref[:, ns]
        cg = c_ref[:, ns]
        xg = x[:, gs]
        cb = lax.dot_general(cg, bg, (((1,), (1,)), ((), ())), preferred_element_type=F32)
        h_prev = h_ref[:, gs]
        y = jnp.dot(cg, h_prev.astype(BF16), preferred_element_type=F32) * ea_exp[:, gs]
        bg_t = bg.astype(F32).T.astype(BF16)
        st = jnp.dot(bg_t, xw[:, gs], preferred_element_type=F32)
        h_ref[:, gs] = h_prev * cd_exp[:, gs] + st
        for j in range(SSM_HPG):
            hd = g * SSM_HPG + j
            seg = acum[:, hd:hd + 1] - acum_t[hd:hd + 1, :]
            m = cb * jnp.exp(jnp.where(mask, seg, NEG_BIG)) * dt_t[hd:hd + 1, :]
            xm = jnp.where(glane == j, xg, jnp.zeros_like(xg))
            y = y + jnp.dot(m.astype(BF16), xm, preferred_element_type=F32)
        y_ref[:, gs] = y.astype(y_ref.dtype)


def _ssd(small, prm, u):
    s = u.shape[0]
    n = s // SSD_CHUNK
    q = SSD_CHUNK
    bc_block = SSM_INNER // (SSM_GROUPS * SSM_STATE)
    return pl.pallas_call(
        _ssd_kernel,
        grid=(2, n),
        in_specs=[pl.BlockSpec((q, LANES), lambda d, c: (_chunk_index(d, c, n), d)),
                  pl.BlockSpec((None, 8, LANES), lambda d, c: (d, 0, 0)),
                  pl.BlockSpec((q, SSM_INNER), lambda d, c: (_chunk_index(d, c, n), 0)),
                  pl.BlockSpec((q, SSM_GROUPS * SSM_STATE), lambda d, c: (_chunk_index(d, c, n), bc_block)),
                  pl.BlockSpec((q, SSM_GROUPS * SSM_STATE), lambda d, c: (_chunk_index(d, c, n), bc_block + 1))],
        out_specs=pl.BlockSpec((None, q, SSM_INNER), lambda d, c: (d, _chunk_index(d, c, n), 0)),
        out_shape=jax.ShapeDtypeStruct((2, s, SSM_INNER), F32),
        scratch_shapes=[pltpu.VMEM((SSM_STATE, SSM_INNER), F32)],
        compiler_params=_params(dimension_semantics=("arbitrary", "arbitrary")),
        name="ssd",
    )(small, prm, u, u, u)


def _dn_kernel(sm_ref, prm_ref, q_ref, k_ref, v_ref, o_ref, s_ref):
    d = pl.program_id(0)
    c = DN_CHUNK

    @pl.when(pl.program_id(1) == 0)
    def _():
        s_ref[...] = jnp.zeros_like(s_ref)

    lane = lax.broadcasted_iota(jnp.int32, (1, LANES), 1)
    prm = prm_ref[...]
    sm = sm_ref[...]
    a_row = jnp.where(jnp.logical_and(lane >= SM_A, lane < SM_B), -jnp.exp(prm[1:2, :]), 0.0)
    g = a_row * _softplus(sm + prm[0:1, :])
    beta = _sigmoid(sm)
    gcum = jnp.dot(_tri(d, c).astype(F32), g, preferred_element_type=F32,
                   precision=lax.Precision.HIGHEST)
    glast = jnp.sum(g, axis=0, keepdims=True)
    gcum_t2 = jnp.concatenate([gcum, gcum], axis=0).T
    eg = jnp.exp(gcum)

    row = lax.broadcasted_iota(jnp.int32, (c, LANES), 0)
    col = lax.broadcasted_iota(jnp.int32, (c, LANES), 1)
    tok = jnp.bitwise_and(col, c - 1)
    ahead = (row - tok) * (1 - 2 * d)
    incl = ahead >= 0
    strict = ahead > 0
    left = col < c
    eye_right = jnp.where(jnp.logical_and(row == tok, jnp.logical_not(left)), 1.0, 0.0)
    zc = jnp.zeros((c, LANES), BF16)

    for kh in range(DN_K_HEADS):
        ks = slice(kh * DN_HEAD, (kh + 1) * DN_HEAD)
        qh = q_ref[:, ks]
        kk_in = k_ref[:, ks]
        k2 = jnp.concatenate([kk_in, kk_in], axis=0)
        nt = (((1,), (1,)), ((), ()))
        kk2 = lax.dot_general(kk_in, k2, nt, preferred_element_type=F32)
        qk2 = lax.dot_general(qh, k2, nt, preferred_element_type=F32)
        kf = kk_in.astype(F32)
        qf = qh.astype(F32)
        k_t = jnp.concatenate([kf, jnp.zeros_like(kf)], axis=0).T
        for r in range(DN_V_HEADS // DN_K_HEADS):
            hv = kh * (DN_V_HEADS // DN_K_HEADS) + r
            la = SM_A + hv
            lb = SM_B + hv
            vs = slice(hv * DN_HEAD, (hv + 1) * DN_HEAD)
            gc_col = gcum[:, la:la + 1]
            gc_row = gcum_t2[la:la + 1, :]
            beta_col = beta[:, lb:lb + 1]
            eg_col = eg[:, la:la + 1]
            gl = glast[:, la:la + 1]
            dec = jnp.exp(jnp.where(incl, gc_col - gc_row, NEG_BIG))
            x2 = jnp.where(strict, kk2 * (-beta_col) * dec, 0.0)
            attn = qk2 * dec
            vb = v_ref[:, vs].astype(F32) * beta_col
            kb = kf * (beta_col * eg_col)
            rhs_in = jnp.concatenate([vb, kb], axis=1).astype(BF16)
            rhs_pad = jnp.concatenate([jnp.zeros_like(rhs_in), rhs_in], axis=0)

            lhs = jnp.where(left, x2, 0.0)
            rhs = lhs + eye_right
            tpad = eye_right
            for _ in range(NEUMANN_LEVELS):
                res = jnp.dot(lhs.astype(BF16), jnp.concatenate([rhs.astype(BF16), zc], axis=0),
                              preferred_element_type=F32)
                rhs = res + tpad
                tpad = jnp.where(left, 0.0, rhs)
                lhs = jnp.where(left, res, 0.0)
            uw = jnp.dot(tpad.astype(BF16), rhs_pad, preferred_element_type=F32)
            u = uw[:, :DN_HEAD]
            w = uw[:, DN_HEAD:]

            state = s_ref[hv]
            wq = jnp.concatenate([w, qf * eg_col], axis=0).astype(BF16)
            ws_qs = jnp.dot(wq, state.astype(BF16), preferred_element_type=F32)
            v_new = u - ws_qs[:c]
            vn_pad = jnp.concatenate([v_new.astype(BF16), zc], axis=0)
            o = ws_qs[c:] + jnp.dot(attn.astype(BF16), vn_pad, preferred_element_type=F32)
            kd_t = (k_t * jnp.exp(gl - gc_row)).astype(BF16)
            s_ref[hv] = state * jnp.exp(gl) + jnp.dot(kd_t, vn_pad, preferred_element_type=F32)
            o_ref[:, vs] = o.astype(o_ref.dtype)


def _dn(small, prm, u):
    s = u.shape[0]
    n = s // DN_CHUNK
    c = DN_CHUNK
    qb = (SSM_INNER + 2 * SSM_GROUPS * SSM_STATE) // DN_KEY_DIM
    vb = (CONV_DIM - DN_VAL_DIM) // DN_VAL_DIM
    return pl.pallas_call(
        _dn_kernel,
        grid=(2, n),
        in_specs=[pl.BlockSpec((c, LANES), lambda d, i: (_chunk_index(d, i, n), d)),
                  pl.BlockSpec((None, 8, LANES), lambda d, i: (d, 0, 0)),
                  pl.BlockSpec((c, DN_KEY_DIM), lambda d, i: (_chunk_index(d, i, n), qb)),
                  pl.BlockSpec((c, DN_KEY_DIM), lambda d, i: (_chunk_index(d, i, n), qb + 1)),
                  pl.BlockSpec((c, DN_VAL_DIM), lambda d, i: (_chunk_index(d, i, n), vb))],
        out_specs=pl.BlockSpec((None, c, DN_VAL_DIM), lambda d, i: (d, _chunk_index(d, i, n), 0)),
        out_shape=jax.ShapeDtypeStruct((2, s, DN_VAL_DIM), F32),
        scratch_shapes=[pltpu.VMEM((DN_V_HEADS, DN_HEAD, DN_HEAD), F32)],
        compiler_params=_params(dimension_semantics=("arbitrary", "arbitrary")),
        name="dn",
    )(small, prm, u, u, u)


OUT_TM = 256
Z_SSM_BLOCK = CONV_DIM // D_MODEL
Z_DN_BLOCK = Z_SSM_BLOCK + 1
G_SSM_BLOCK = Z_SSM_BLOCK + 2
G_DN_BLOCK = Z_SSM_BLOCK + 3


def _ssm_out_kernel(yf_ref, yb_ref, x_ref, z_ref, dsk_ref, nw_ref, w_ref, o_ref):
    z = z_ref[...].astype(F32)
    y = (yf_ref[...] + yb_ref[...] + dsk_ref[...] * x_ref[...].astype(F32)) * (z * _sigmoid(z))
    y = y * lax.rsqrt(jnp.mean(y * y, axis=-1, keepdims=True) + RMS_EPS) * nw_ref[...]
    o_ref[...] = jnp.dot(y.astype(BF16), w_ref[...], preferred_element_type=F32).astype(o_ref.dtype)


def _dn_out_kernel(of_ref, ob_ref, z_ref, nw_ref, w_ref, o_ref, y_ref):
    nw = nw_ref[...]
    for h in range(DN_V_HEADS):
        hs = slice(h * DN_HEAD, (h + 1) * DN_HEAD)
        o = of_ref[:, hs] + ob_ref[:, hs]
        z = z_ref[:, hs].astype(F32)
        o = o * lax.rsqrt(jnp.mean(o * o, axis=-1, keepdims=True) + RMS_EPS) * nw
        y_ref[:, hs] = (o * (z * _sigmoid(z))).astype(BF16)
    o_ref[...] = jnp.dot(y_ref[...], w_ref[...], preferred_element_type=F32).astype(o_ref.dtype)


def _merge_kernel(ps_ref, pd_ref, gs_ref, gd_ref, w_ref, x_ref, gate_ref, lg_ref, lb_ref, o_ref):
    mix = (_sigmoid(gs_ref[...].astype(F32)) * ps_ref[...].astype(F32)
           + _sigmoid(gd_ref[...].astype(F32)) * pd_ref[...].astype(F32))
    mixed = jnp.dot(mix.astype(BF16), w_ref[...], preferred_element_type=F32)
    r = DEEPNORM_ALPHA * x_ref[...] + gate_ref[...] * mixed
    mu = jnp.mean(r, axis=-1, keepdims=True)
    rc = r - mu
    var = jnp.mean(rc * rc, axis=-1, keepdims=True)
    o_ref[...] = rc * lax.rsqrt(var + LN_EPS) * lg_ref[...] + lb_ref[...]


def _row_spec(tm, width, block_col=0):
    return pl.BlockSpec((tm, width), lambda i: (i, block_col))


def _const_spec(shape):
    return pl.BlockSpec(shape, lambda i: (0,) * len(shape))


def _dir_spec(tm, width, d):
    return pl.BlockSpec((None, tm, width), lambda i: (d, i, 0))


def _ssm_out(y, u, proj, d_skip, norm_w, w):
    s = u.shape[0]
    tm = OUT_TM
    return pl.pallas_call(
        _ssm_out_kernel,
        grid=(s // tm,),
        in_specs=[_dir_spec(tm, SSM_INNER, 0), _dir_spec(tm, SSM_INNER, 1),
                  _row_spec(tm, SSM_INNER, 0), _row_spec(tm, SSM_INNER, Z_SSM_BLOCK),
                  _const_spec((1, SSM_INNER)), _const_spec((1, SSM_INNER)),
                  _const_spec((SSM_INNER, D_MODEL))],
        out_specs=_row_spec(tm, D_MODEL),
        out_shape=jax.ShapeDtypeStruct((s, D_MODEL), BF16),
        compiler_params=_params(dimension_semantics=("arbitrary",)),
        name="ssm_out",
    )(y, y, u, proj, d_skip, norm_w, w)


def _dn_out(o, proj, norm_w, w):
    s = proj.shape[0]
    tm = OUT_TM
    return pl.pallas_call(
        _dn_out_kernel,
        grid=(s // tm,),
        in_specs=[_dir_spec(tm, DN_VAL_DIM, 0), _dir_spec(tm, DN_VAL_DIM, 1),
                  _row_spec(tm, DN_VAL_DIM, Z_DN_BLOCK),
                  _const_spec((1, DN_HEAD)), _const_spec((DN_VAL_DIM, D_MODEL))],
        out_specs=_row_spec(tm, D_MODEL),
        out_shape=jax.ShapeDtypeStruct((s, D_MODEL), BF16),
        scratch_shapes=[pltpu.VMEM((tm, DN_VAL_DIM), BF16)],
        compiler_params=_params(dimension_semantics=("arbitrary",)),
        name="dn_out",
    )(o, o, proj, norm_w, w)


def _merge(p_ssm, p_dn, proj, w_out, x2, gate, ln_g, ln_b):
    s = x2.shape[0]
    tm = OUT_TM
    return pl.pallas_call(
        _merge_kernel,
        grid=(s // tm,),
        in_specs=[_row_spec(tm, D_MODEL), _row_spec(tm, D_MODEL),
                  _row_spec(tm, D_MODEL, G_SSM_BLOCK), _row_spec(tm, D_MODEL, G_DN_BLOCK),
                  _const_spec((D_MODEL, D_MODEL)), _row_spec(tm, D_MODEL),
                  _const_spec((1, D_MODEL)), _const_spec((1, D_MODEL)), _const_spec((1, D_MODEL))],
        out_specs=_row_spec(tm, D_MODEL),
        out_shape=jax.ShapeDtypeStruct((s, D_MODEL), F32),
        compiler_params=_params(dimension_semantics=("arbitrary",)),
        name="merge",
    )(p_ssm, p_dn, proj, proj, w_out, x2, gate, ln_g, ln_b)


def _small_weight(w_in):
    base = CONV_DIM + SSM_INNER + DN_VAL_DIM
    dt0 = base
    a0 = dt0 + 2 * SSM_HEADS
    b0 = a0 + 2 * DN_V_HEADS
    tiles = []
    for d in range(2):
        tiles += [w_in[:, dt0 + d * SSM_HEADS:dt0 + (d + 1) * SSM_HEADS],
                  w_in[:, a0 + d * DN_V_HEADS:a0 + (d + 1) * DN_V_HEADS],
                  w_in[:, b0 + d * DN_V_HEADS:b0 + (d + 1) * DN_V_HEADS],
                  jnp.zeros((D_MODEL, LANES - SSM_HEADS - 2 * DN_V_HEADS), w_in.dtype)]
    return jnp.concatenate(tiles, axis=1)


def _dir_params(ssm_dt_bias, dn_dt_bias, ssm_a_log, dn_a_log):
    pad = jnp.zeros((2, LANES - SSM_HEADS - DN_V_HEADS), F32)
    bias = jnp.concatenate([ssm_dt_bias, dn_dt_bias, pad], axis=1)
    alog = jnp.concatenate([ssm_a_log, dn_a_log, pad], axis=1)
    rest = jnp.zeros((2, 6, LANES), F32)
    return jnp.concatenate([bias[:, None, :], alog[:, None, :], rest], axis=1)


def kernel(x, c, w_ada, b_ada, w_in, conv_w, conv_b, ssm_a_log, ssm_dt_bias, ssm_d, ssm_norm_w,
           dn_a_log, dn_dt_bias, dn_norm_w, w_branch_ssm, w_branch_dn, w_out, ln_g, ln_b):
    bsz, s, dm = x.shape
    assert bsz == 1 and dm == D_MODEL and w_ada.shape[0] == 1
    x2 = x[0]

    mod = _adaln(jnp.broadcast_to(c, (8, D_MODEL)), w_ada[0], b_ada)
    shift = mod[0:1, 0:D_MODEL]
    scale = mod[0:1, D_MODEL:2 * D_MODEL]
    gate = mod[0:1, 2 * D_MODEL:3 * D_MODEL]

    n_main = CONV_DIM + SSM_INNER + DN_VAL_DIM
    w_main = jnp.concatenate([w_in[0][:, :n_main], w_in[0][:, n_main + N_SMALL:]], axis=1).astype(BF16)
    w_small = _small_weight(w_in[0]).astype(BF16)
    proj = _inproj(x2, scale, shift, w_main, BF16, 1024, 1024, "inproj")
    small = _inproj(x2, scale, shift, w_small, F32, 1024, 2 * LANES, "inproj_small")

    u = _conv(proj, conv_w[0], conv_b)
    prm = _dir_params(ssm_dt_bias[0], dn_dt_bias[0], ssm_a_log[0], dn_a_log[0])
    y = _ssd(small, prm, u)
    o = _dn(small, prm, u)

    d_skip = jnp.repeat(ssm_d[0], SSM_HEAD_DIM)[None, :]
    p_ssm = _ssm_out(y, u, proj, d_skip, ssm_norm_w, w_branch_ssm[0].astype(BF16))
    p_dn = _dn_out(o, proj, dn_norm_w, w_branch_dn[0].astype(BF16))
    out = _merge(p_ssm, p_dn, proj, w_out[0].astype(BF16), x2, gate, ln_g, ln_b)
    return out[None]
```

```python
import functools

import jax
import jax.numpy as jnp
from jax import lax
from jax.experimental import pallas as pl
from jax.experimental.pallas import tpu as pltpu

F32 = jnp.float32
BF16 = jnp.bfloat16

D_MODEL = 2048
SSM_HEADS = 32
SSM_HEAD_DIM = 64
SSM_GROUPS = 8
SSM_HPG = SSM_HEADS // SSM_GROUPS
SSM_STATE = 128
SSM_INNER = SSM_HEADS * SSM_HEAD_DIM
DN_K_HEADS = 8
DN_V_HEADS = 16
DN_HEAD = 128
DN_KEY_DIM = DN_K_HEADS * DN_HEAD
DN_VAL_DIM = DN_V_HEADS * DN_HEAD
CONV_WIDTH = 5
CONV_DIM = SSM_INNER + 2 * SSM_GROUPS * SSM_STATE + 2 * DN_KEY_DIM + DN_VAL_DIM
N_SMALL = 2 * SSM_HEADS + 4 * DN_V_HEADS
DEEPNORM_ALPHA = 2.0 ** 0.25
RMS_EPS = 1e-6
LN_EPS = 1e-5

LANES = 128
SSD_CHUNK = 128
DN_CHUNK = 64
NEUMANN_LEVELS = 6
HALO = 16
NEG_BIG = -1e30
VMEM_LIMIT = 56 * 1024 * 1024

SM_DT = 0
SM_A = SSM_HEADS
SM_B = SSM_HEADS + DN_V_HEADS


def _sigmoid(x):
    return 1.0 / (1.0 + jnp.exp(-x))


def _softplus(x):
    return jnp.maximum(x, 0.0) + jnp.log(1.0 + jnp.exp(-jnp.abs(x)))


def _params(**kw):
    return pltpu.CompilerParams(vmem_limit_bytes=VMEM_LIMIT, **kw)


def _adaln_kernel(c_ref, w_ref, b_ref, o_ref):
    o_ref[...] = jnp.dot(c_ref[...], w_ref[...], preferred_element_type=F32,
                         precision=lax.Precision.HIGHEST) + b_ref[...]


def _adaln(c8, w_ada, b_ada):
    n = w_ada.shape[1]
    tn = 768
    return pl.pallas_call(
        _adaln_kernel,
        grid=(n // tn,),
        in_specs=[pl.BlockSpec((8, D_MODEL), lambda j: (0, 0)),
                  pl.BlockSpec((D_MODEL, tn), lambda j: (0, j)),
                  pl.BlockSpec((1, tn), lambda j: (0, j))],
        out_specs=pl.BlockSpec((8, tn), lambda j: (0, j)),
        out_shape=jax.ShapeDtypeStruct((8, n), F32),
        compiler_params=_params(dimension_semantics=("arbitrary",)),
        name="adaln",
    )(c8, w_ada, b_ada)


def _inproj_kernel(x_ref, sc_ref, sh_ref, w_ref, o_ref, h_ref):
    @pl.when(pl.program_id(1) == 0)
    def _():
        h_ref[...] = (x_ref[...] * (1.0 + sc_ref[...]) + sh_ref[...]).astype(BF16)

    o_ref[...] = jnp.dot(h_ref[...], w_ref[...], preferred_element_type=F32).astype(o_ref.dtype)


def _inproj(x2, scale, shift, w, out_dtype, tm, tn, name):
    s = x2.shape[0]
    n = w.shape[1]
    return pl.pallas_call(
        _inproj_kernel,
        grid=(s // tm, n // tn),
        in_specs=[pl.BlockSpec((tm, D_MODEL), lambda i, j: (i, 0)),
                  pl.BlockSpec((1, D_MODEL), lambda i, j: (0, 0)),
                  pl.BlockSpec((1, D_MODEL), lambda i, j: (0, 0)),
                  pl.BlockSpec((D_MODEL, tn), lambda i, j: (0, j))],
        out_specs=pl.BlockSpec((tm, tn), lambda i, j: (i, j)),
        out_shape=jax.ShapeDtypeStruct((s, n), out_dtype),
        scratch_shapes=[pltpu.VMEM((tm, D_MODEL), BF16)],
        compiler_params=_params(dimension_semantics=("arbitrary", "arbitrary")),
        name=name,
    )(x2, scale, shift, w)


CONV_T = 512
CONV_TC = 1024
CONV_RB = 64
Q_BLOCK = (SSM_INNER + 2 * SSM_GROUPS * SSM_STATE) // CONV_TC
K_BLOCK = Q_BLOCK + DN_KEY_DIM // CONV_TC


def _conv_kernel(prev_ref, main_ref, next_ref, w_ref, b_ref, o_ref, ext_ref):
    i = pl.program_id(0)
    j = pl.program_id(1)
    has_prev = (i > 0).astype(F32)
    has_next = (i < pl.num_programs(0) - 1).astype(F32)
    ext_ref[0:HALO, :] = prev_ref[...].astype(F32) * has_prev
    ext_ref[HALO:HALO + CONV_T, :] = main_ref[...].astype(F32)
    ext_ref[HALO + CONV_T:2 * HALO + CONV_T, :] = next_ref[...].astype(F32) * has_next
    w = w_ref[...]
    b = b_ref[...]
    pad = (CONV_WIDTH - 1) // 2
    is_q = j == Q_BLOCK
    is_qk = jnp.logical_or(is_q, j == K_BLOCK)
    qk_scale = jnp.where(is_q, DN_HEAD ** -0.5, 1.0).astype(F32)

    for r in range(CONV_T // CONV_RB):
        r0 = r * CONV_RB
        acc = b
        for t in range(CONV_WIDTH):
            acc = acc + w[t:t + 1, :] * ext_ref[HALO + r0 + t - pad:HALO + r0 + t - pad + CONV_RB, :]
        u = acc * _sigmoid(acc)

        @pl.when(is_qk)
        def _():
            for hh in range(CONV_TC // DN_HEAD):
                uh = u[:, hh * DN_HEAD:(hh + 1) * DN_HEAD]
                inv = lax.rsqrt(jnp.sum(uh * uh, axis=-1, keepdims=True) + 1e-6) * qk_scale
                o_ref[r0:r0 + CONV_RB, hh * DN_HEAD:(hh + 1) * DN_HEAD] = (uh * inv).astype(o_ref.dtype)

        @pl.when(jnp.logical_not(is_qk))
        def _():
            o_ref[r0:r0 + CONV_RB, :] = u.astype(o_ref.dtype)


def _conv(proj, conv_w, conv_b):
    s = proj.shape[0]
    nt = s // CONV_T
    hb = CONV_T // HALO
    last_halo = s // HALO - 1
    return pl.pallas_call(
        _conv_kernel,
        grid=(nt, CONV_DIM // CONV_TC),
        in_specs=[pl.BlockSpec((HALO, CONV_TC), lambda i, j: (jnp.maximum(i * hb - 1, 0), j)),
                  pl.BlockSpec((CONV_T, CONV_TC), lambda i, j: (i, j)),
                  pl.BlockSpec((HALO, CONV_TC), lambda i, j: (jnp.minimum((i + 1) * hb, last_halo), j)),
                  pl.BlockSpec((CONV_WIDTH, CONV_TC), lambda i, j: (0, j)),
                  pl.BlockSpec((1, CONV_TC), lambda i, j: (0, j))],
        out_specs=pl.BlockSpec((CONV_T, CONV_TC), lambda i, j: (i, j)),
        out_shape=jax.ShapeDtypeStruct((s, CONV_DIM), BF16),
        scratch_shapes=[pltpu.VMEM((CONV_T + 2 * HALO, CONV_TC), F32)],
        compiler_params=_params(dimension_semantics=("arbitrary", "arbitrary")),
        name="conv",
    )(proj, proj, proj, conv_w, conv_b)


def _chunk_index(d, c, n):
    return c + d * (n - 1 - 2 * c)


def _tri(d, n):
    row = lax.broadcasted_iota(jnp.int32, (n, n), 0)
    col = lax.broadcasted_iota(jnp.int32, (n, n), 1)
    return (row - col) * (1 - 2 * d) >= 0


def _ssd_kernel(sm_ref, prm_ref, x_ref, b_ref, c_ref, y_ref, h_ref):
    d = pl.program_id(0)
    q = SSD_CHUNK

    @pl.when(pl.program_id(1) == 0)
    def _():
        h_ref[...] = jnp.zeros_like(h_ref)

    lane = lax.broadcasted_iota(jnp.int32, (1, LANES), 1)
    prm = prm_ref[...]
    a_row = jnp.where(lane < SSM_HEADS, -jnp.exp(prm[1:2, :]), 0.0)
    dt = _softplus(sm_ref[...] + prm[0:1, :])
    dta = dt * a_row
    mask = _tri(d, q)
    acum = jnp.dot(mask.astype(F32), dta, preferred_element_type=F32,
                   precision=lax.Precision.HIGHEST)
    total = jnp.sum(dta, axis=0, keepdims=True)
    acum_t = acum.T
    dt_t = dt.T

    erow = lax.broadcasted_iota(jnp.int32, (LANES, SSM_INNER), 0)
    ecol = lax.broadcasted_iota(jnp.int32, (LANES, SSM_INNER), 1)
    expand = (erow == ecol // SSM_HEAD_DIM).astype(BF16)

    def expand_heads(v):
        hi = v.astype(BF16)
        lo = (v - hi.astype(F32)).astype(BF16)
        return (jnp.dot(hi, expand, preferred_element_type=F32)
                + jnp.dot(lo, expand, preferred_element_type=F32))

    w_exp = expand_heads(dt * jnp.exp(total - acum))
    ea_exp = expand_heads(jnp.exp(acum))
    cd_exp = expand_heads(jnp.broadcast_to(jnp.exp(total), (8, LANES)))[0:1, :]

    x = x_ref[...]
    xw = (x.astype(F32) * w_exp).astype(BF16)
    glane = lax.broadcasted_iota(jnp.int32, (1, SSM_HPG * SSM_HEAD_DIM), 1) // SSM_HEAD_DIM

    for g in range(SSM_GROUPS):
        gs = slice(g * SSM_HPG * SSM_HEAD_DIM, (g + 1) * SSM_HPG * SSM_HEAD_DIM)
        ns = slice(g * SSM_STATE, (g + 1) * SSM_STATE)
        bg = b_ref[:, ns]
        cg = c_ref[:, ns]
        xg = x[:, gs]
        cb = lax.dot_general(cg, bg, (((1,), (1,)), ((), ())), preferred_element_type=F32)
        h_prev = h_ref[:, gs]
        y = jnp.dot(cg, h_prev.astype(BF16), preferred_element_type=F32) * ea_exp[:, gs]
        bg_t = bg.astype(F32).T.astype(BF16)
        st = jnp.dot(bg_t, xw[:, gs], preferred_element_type=F32)
        h_ref[:, gs] = h_prev * cd_exp[:, gs] + st
        for j in range(SSM_HPG):
            hd = g * SSM_HPG + j
            seg = acum[:, hd:hd + 1] - acum_t[hd:hd + 1, :]
            m = cb * jnp.exp(jnp.where(mask, seg, NEG_BIG)) * dt_t[hd:hd + 1, :]
            xm = jnp.where(glane == j, xg, jnp.zeros_like(xg))
            y = y + jnp.dot(m.astype(BF16), xm, preferred_element_type=F32)
        y_ref[:, gs] = y.astype(y_ref.dtype)


def _ssd(small, prm, u):
    s = u.shape[0]
    n = s // SSD_CHUNK
    q = SSD_CHUNK
    bc_block = SSM_INNER // (SSM_GROUPS * SSM_STATE)
    return pl.pallas_call(
        _ssd_kernel,
        grid=(2, n),
        in_specs=[pl.BlockSpec((q, LANES), lambda d, c: (_chunk_index(d, c, n), d)),
                  pl.BlockSpec((None, 8, LANES), lambda d, c: (d, 0, 0)),
                  pl.BlockSpec((q, SSM_INNER), lambda d, c: (_chunk_index(d, c, n), 0)),
                  pl.BlockSpec((q, SSM_GROUPS * SSM_STATE), lambda d, c: (_chunk_index(d, c, n), bc_block)),
                  pl.BlockSpec((q, SSM_GROUPS * SSM_STATE), lambda d, c: (_chunk_index(d, c, n), bc_block + 1))],
        out_specs=pl.BlockSpec((None, q, SSM_INNER), lambda d, c: (d, _chunk_index(d, c, n), 0)),
        out_shape=jax.ShapeDtypeStruct((2, s, SSM_INNER), F32),
        scratch_shapes=[pltpu.VMEM((SSM_STATE, SSM_INNER), F32)],
        compiler_params=_params(dimension_semantics=("arbitrary", "arbitrary")),
        name="ssd",
    )(small, prm, u, u, u)


def _dn_kernel(sm_ref, prm_ref, q_ref, k_ref, v_ref, o_ref, s_ref):
    d = pl.program_id(0)
    c = DN_CHUNK

    @pl.when(pl.program_id(1) == 0)
    def _():
        s_ref[...] = jnp.zeros_like(s_ref)

    lane = lax.broadcasted_iota(jnp.int32, (1, LANES), 1)
    prm = prm_ref[...]
    sm = sm_ref[...]
    a_row = jnp.where(jnp.logical_and(lane >= SM_A, lane < SM_B), -jnp.exp(prm[1:2, :]), 0.0)
    g = a_row * _softplus(sm + prm[0:1, :])
    beta = _sigmoid(sm)
    gcum = jnp.dot(_tri(d, c).astype(F32), g, preferred_element_type=F32,
                   precision=lax.Precision.HIGHEST)
    glast = jnp.sum(g, axis=0, keepdims=True)
    gcum_t2 = jnp.concatenate([gcum, gcum], axis=0).T
    eg = jnp.exp(gcum)

    row = lax.broadcasted_iota(jnp.int32, (c, LANES), 0)
    col = lax.broadcasted_iota(jnp.int32, (c, LANES), 1)
    tok = jnp.bitwise_and(col, c - 1)
    ahead = (row - tok) * (1 - 2 * d)
    incl = ahead >= 0
    strict = ahead > 0
    left = col < c
    eye_right = jnp.where(jnp.logical_and(row == tok, jnp.logical_not(left)), 1.0, 0.0)
    zc = jnp.zeros((c, LANES), BF16)

    rep = DN_V_HEADS // DN_K_HEADS
    heads = range(DN_V_HEADS)
    nt = (((1,), (1,)), ((), ()))
    kk2, qk2, kf, qf, k_t = [], [], [], [], []
    for kh in range(DN_K_HEADS):
        ks = slice(kh * DN_HEAD, (kh + 1) * DN_HEAD)
        qh = q_ref[:, ks]
        kin = k_ref[:, ks]
        k2 = jnp.concatenate([kin, kin], axis=0)
        kk2.append(lax.dot_general(kin, k2, nt, preferred_element_type=F32))
        qk2.append(lax.dot_general(qh, k2, nt, preferred_element_type=F32))
        kf.append(kin.astype(F32))
        qf.append(qh.astype(F32))
        k_t.append(jnp.concatenate([kf[kh], jnp.zeros_like(kf[kh])], axis=0).T)

    gc_row, eg_col, gl, attn, rhs_pad, lhs, rhs, tpad = [], [], [], [], [], [], [], []
    for hv in heads:
        kh = hv // rep
        la = SM_A + hv
        lb = SM_B + hv
        gc_col = gcum[:, la:la + 1]
        gc_row.append(gcum_t2[la:la + 1, :])
        beta_col = beta[:, lb:lb + 1]
        eg_col.append(eg[:, la:la + 1])
        gl.append(glast[:, la:la + 1])
        dec = jnp.exp(jnp.where(incl, gc_col - gc_row[hv], NEG_BIG))
        x2 = jnp.where(strict, kk2[kh] * (-beta_col) * dec, 0.0)
        attn.append((qk2[kh] * dec).astype(BF16))
        vb = v_ref[:, hv * DN_HEAD:(hv + 1) * DN_HEAD].astype(F32) * beta_col
        kb = kf[kh] * (beta_col * eg_col[hv])
        rhs_in = jnp.concatenate([vb, kb], axis=1).astype(BF16)
        rhs_pad.append(jnp.concatenate([jnp.zeros_like(rhs_in), rhs_in], axis=0))
        lhs.append(jnp.where(left, x2, 0.0))
        rhs.append(lhs[hv] + eye_right)
        tpad.append(eye_right)

    for _ in range(NEUMANN_LEVELS):
        for hv in heads:
            res = jnp.dot(lhs[hv].astype(BF16), jnp.concatenate([rhs[hv].astype(BF16), zc], axis=0),
                          preferred_element_type=F32)
            rhs[hv] = res + tpad[hv]
            tpad[hv] = jnp.where(left, 0.0, rhs[hv])
            lhs[hv] = jnp.where(left, res, 0.0)

    uw = [jnp.dot(tpad[hv].astype(BF16), rhs_pad[hv], preferred_element_type=F32) for hv in heads]
    state = [s_ref[hv] for hv in heads]
    ws_qs = []
    for hv in heads:
        wq = jnp.concatenate([uw[hv][:, DN_HEAD:], qf[hv // rep] * eg_col[hv]], axis=0).astype(BF16)
        ws_qs.append(jnp.dot(wq, state[hv].astype(BF16), preferred_element_type=F32))
    vn_pad = [jnp.concatenate([(uw[hv][:, :DN_HEAD] - ws_qs[hv][:c]).astype(BF16), zc], axis=0) for hv in heads]
    for hv in heads:
        o = ws_qs[hv][c:] + jnp.dot(attn[hv], vn_pad[hv], preferred_element_type=F32)
        o_ref[:, hv * DN_HEAD:(hv + 1) * DN_HEAD] = o.astype(o_ref.dtype)
    for hv in heads:
        kd_t = (k_t[hv // rep] * jnp.exp(gl[hv] - gc_row[hv])).astype(BF16)
        s_ref[hv] = state[hv] * jnp.exp(gl[hv]) + jnp.dot(kd_t, vn_pad[hv], preferred_element_type=F32)


def _dn(small, prm, u):
    s = u.shape[0]
    n = s // DN_CHUNK
    c = DN_CHUNK
    qb = (SSM_INNER + 2 * SSM_GROUPS * SSM_STATE) // DN_KEY_DIM
    vb = (CONV_DIM - DN_VAL_DIM) // DN_VAL_DIM
    return pl.pallas_call(
        _dn_kernel,
        grid=(2, n),
        in_specs=[pl.BlockSpec((c, LANES), lambda d, i: (_chunk_index(d, i, n), d)),
                  pl.BlockSpec((None, 8, LANES), lambda d, i: (d, 0, 0)),
                  pl.BlockSpec((c, DN_KEY_DIM), lambda d, i: (_chunk_index(d, i, n), qb)),
                  pl.BlockSpec((c, DN_KEY_DIM), lambda d, i: (_chunk_index(d, i, n), qb + 1)),
                  pl.BlockSpec((c, DN_VAL_DIM), lambda d, i: (_chunk_index(d, i, n), vb))],
        out_specs=pl.BlockSpec((None, c, DN_VAL_DIM), lambda d, i: (d, _chunk_index(d, i, n), 0)),
        out_shape=jax.ShapeDtypeStruct((2, s, DN_VAL_DIM), F32),
        scratch_shapes=[pltpu.VMEM((DN_V_HEADS, DN_HEAD, DN_HEAD), F32)],
        compiler_params=_params(dimension_semantics=("arbitrary", "arbitrary")),
        name="dn",
    )(small, prm, u, u, u)


OUT_TM = 256
Z_SSM_BLOCK = CONV_DIM // D_MODEL
Z_DN_BLOCK = Z_SSM_BLOCK + 1
G_SSM_BLOCK = Z_SSM_BLOCK + 2
G_DN_BLOCK = Z_SSM_BLOCK + 3


def _ssm_out_kernel(yf_ref, yb_ref, x_ref, z_ref, dsk_ref, nw_ref, w_ref, o_ref):
    z = z_ref[...].astype(F32)
    y = (yf_ref[...] + yb_ref[...] + dsk_ref[...] * x_ref[...].astype(F32)) * (z * _sigmoid(z))
    y = y * lax.rsqrt(jnp.mean(y * y, axis=-1, keepdims=True) + RMS_EPS) * nw_ref[...]
    o_ref[...] = jnp.dot(y.astype(BF16), w_ref[...], preferred_element_type=F32).astype(o_ref.dtype)


def _dn_out_kernel(of_ref, ob_ref, z_ref, nw_ref, w_ref, o_ref, y_ref):
    nw = nw_ref[...]
    for h in range(DN_V_HEADS):
        hs = slice(h * DN_HEAD, (h + 1) * DN_HEAD)
        o = of_ref[:, hs] + ob_ref[:, hs]
        z = z_ref[:, hs].astype(F32)
        o = o * lax.rsqrt(jnp.mean(o * o, axis=-1, keepdims=True) + RMS_EPS) * nw
        y_ref[:, hs] = (o * (z * _sigmoid(z))).astype(BF16)
    o_ref[...] = jnp.dot(y_ref[...], w_ref[...], preferred_element_type=F32).astype(o_ref.dtype)


def _merge_kernel(ps_ref, pd_ref, gs_ref, gd_ref, w_ref, x_ref, gate_ref, lg_ref, lb_ref, o_ref):
    mix = (_sigmoid(gs_ref[...].astype(F32)) * ps_ref[...].astype(F32)
           + _sigmoid(gd_ref[...].astype(F32)) * pd_ref[...].astype(F32))
    mixed = jnp.dot(mix.astype(BF16), w_ref[...], preferred_element_type=F32)
    r = DEEPNORM_ALPHA * x_ref[...] + gate_ref[...] * mixed
    mu = jnp.mean(r, axis=-1, keepdims=True)
    rc = r - mu
    var = jnp.mean(rc * rc, axis=-1, keepdims=True)
    o_ref[...] = rc * lax.rsqrt(var + LN_EPS) * lg_ref[...] + lb_ref[...]


def _row_spec(tm, width, block_col=0):
    return pl.BlockSpec((tm, width), lambda i: (i, block_col))


def _const_spec(shape):
    return pl.BlockSpec(shape, lambda i: (0,) * len(shape))


def _dir_spec(tm, width, d):
    return pl.BlockSpec((None, tm, width), lambda i: (d, i, 0))


def _ssm_out(y, u, proj, d_skip, norm_w, w):
    s = u.shape[0]
    tm = OUT_TM
    return pl.pallas_call(
        _ssm_out_kernel,
        grid=(s // tm,),
        in_specs=[_dir_spec(tm, SSM_INNER, 0), _dir_spec(tm, SSM_INNER, 1),
                  _row_spec(tm, SSM_INNER, 0), _row_spec(tm, SSM_INNER, Z_SSM_BLOCK),
                  _const_spec((1, SSM_INNER)), _const_spec((1, SSM_INNER)),
                  _const_spec((SSM_INNER, D_MODEL))],
        out_specs=_row_spec(tm, D_MODEL),
        out_shape=jax.ShapeDtypeStruct((s, D_MODEL), BF16),
        compiler_params=_params(dimension_semantics=("arbitrary",)),
        name="ssm_out",
    )(y, y, u, proj, d_skip, norm_w, w)


def _dn_out(o, proj, norm_w, w):
    s = proj.shape[0]
    tm = OUT_TM
    return pl.pallas_call(
        _dn_out_kernel,
        grid=(s // tm,),
        in_specs=[_dir_spec(tm, DN_VAL_DIM, 0), _dir_spec(tm, DN_VAL_DIM, 1),
                  _row_spec(tm, DN_VAL_DIM, Z_DN_BLOCK),
                  _const_spec((1, DN_HEAD)), _const_spec((DN_VAL_DIM, D_MODEL))],
        out_specs=_row_spec(tm, D_MODEL),
        out_shape=jax.ShapeDtypeStruct((s, D_MODEL), BF16),
        scratch_shapes=[pltpu.VMEM((tm, DN_VAL_DIM), BF16)],
        compiler_params=_params(dimension_semantics=("arbitrary",)),
        name="dn_out",
    )(o, o, proj, norm_w, w)


def _merge(p_ssm, p_dn, proj, w_out, x2, gate, ln_g, ln_b):
    s = x2.shape[0]
    tm = OUT_TM
    return pl.pallas_call(
        _merge_kernel,
        grid=(s // tm,),
        in_specs=[_row_spec(tm, D_MODEL), _row_spec(tm, D_MODEL),
                  _row_spec(tm, D_MODEL, G_SSM_BLOCK), _row_spec(tm, D_MODEL, G_DN_BLOCK),
                  _const_spec((D_MODEL, D_MODEL)), _row_spec(tm, D_MODEL),
                  _const_spec((1, D_MODEL)), _const_spec((1, D_MODEL)), _const_spec((1, D_MODEL))],
        out_specs=_row_spec(tm, D_MODEL),
        out_shape=jax.ShapeDtypeStruct((s, D_MODEL), F32),
        compiler_params=_params(dimension_semantics=("arbitrary",)),
        name="merge",
    )(p_ssm, p_dn, proj, proj, w_out, x2, gate, ln_g, ln_b)


def _small_weight(w_in):
    base = CONV_DIM + SSM_INNER + DN_VAL_DIM
    dt0 = base
    a0 = dt0 + 2 * SSM_HEADS
    b0 = a0 + 2 * DN_V_HEADS
    tiles = []
    for d in range(2):
        tiles += [w_in[:, dt0 + d * SSM_HEADS:dt0 + (d + 1) * SSM_HEADS],
                  w_in[:, a0 + d * DN_V_HEADS:a0 + (d + 1) * DN_V_HEADS],
                  w_in[:, b0 + d * DN_V_HEADS:b0 + (d + 1) * DN_V_HEADS],
                  jnp.zeros((D_MODEL, LANES - SSM_HEADS - 2 * DN_V_HEADS), w_in.dtype)]
    return jnp.concatenate(tiles, axis=1)


def _dir_params(ssm_dt_bias, dn_dt_bias, ssm_a_log, dn_a_log):
    pad = jnp.zeros((2, LANES - SSM_HEADS - DN_V_HEADS), F32)
    bias = jnp.concatenate([ssm_dt_bias, dn_dt_bias, pad], axis=1)
    alog = jnp.concatenate([ssm_a_log, dn_a_log, pad], axis=1)
    rest = jnp.zeros((2, 6, LANES), F32)
    return jnp.concatenate([bias[:, None, :], alog[:, None, :], rest], axis=1)


def kernel(x, c, w_ada, b_ada, w_in, conv_w, conv_b, ssm_a_log, ssm_dt_bias, ssm_d, ssm_norm_w,
           dn_a_log, dn_dt_bias, dn_norm_w, w_branch_ssm, w_branch_dn, w_out, ln_g, ln_b):
    bsz, s, dm = x.shape
    assert bsz == 1 and dm == D_MODEL and w_ada.shape[0] == 1
    x2 = x[0]

    mod = _adaln(jnp.broadcast_to(c, (8, D_MODEL)), w_ada[0], b_ada)
    shift = mod[0:1, 0:D_MODEL]
    scale = mod[0:1, D_MODEL:2 * D_MODEL]
    gate = mod[0:1, 2 * D_MODEL:3 * D_MODEL]

    n_main = CONV_DIM + SSM_INNER + DN_VAL_DIM
    w_main = jnp.concatenate([w_in[0][:, :n_main], w_in[0][:, n_main + N_SMALL:]], axis=1).astype(BF16)
    w_small = _small_weight(w_in[0]).astype(BF16)
    proj = _inproj(x2, scale, shift, w_main, BF16, 1024, 1024, "inproj")
    small = _inproj(x2, scale, shift, w_small, F32, 1024, 2 * LANES, "inproj_small")

    u = _conv(proj, conv_w[0], conv_b)
    prm = _dir_params(ssm_dt_bias[0], dn_dt_bias[0], ssm_a_log[0], dn_a_log[0])
    y = _ssd(small, prm, u)
    o = _dn(small, prm, u)

    d_skip = jnp.repeat(ssm_d[0], SSM_HEAD_DIM)[None, :]
    p_ssm = _ssm_out(y, u, proj, d_skip, ssm_norm_w, w_branch_ssm[0].astype(BF16))
    p_dn = _dn_out(o, proj, dn_norm_w, w_branch_dn[0].astype(BF16))
    out = _merge(p_ssm, p_dn, proj, w_out[0].astype(BF16), x2, gate, ln_g, ln_b)
    return out[None]
```

```python
import functools

import jax
import jax.numpy as jnp
from jax import lax
from jax.experimental import pallas as pl
from jax.experimental.pallas import tpu as pltpu

F32 = jnp.float32
BF16 = jnp.bfloat16

D_MODEL = 2048
SSM_HEADS = 32
SSM_HEAD_DIM = 64
SSM_GROUPS = 8
SSM_HPG = SSM_HEADS // SSM_GROUPS
SSM_STATE = 128
SSM_INNER = SSM_HEADS * SSM_HEAD_DIM
DN_K_HEADS = 8
DN_V_HEADS = 16
DN_HEAD = 128
DN_KEY_DIM = DN_K_HEADS * DN_HEAD
DN_VAL_DIM = DN_V_HEADS * DN_HEAD
CONV_WIDTH = 5
CONV_DIM = SSM_INNER + 2 * SSM_GROUPS * SSM_STATE + 2 * DN_KEY_DIM + DN_VAL_DIM
N_SMALL = 2 * SSM_HEADS + 4 * DN_V_HEADS
DEEPNORM_ALPHA = 2.0 ** 0.25
RMS_EPS = 1e-6
LN_EPS = 1e-5

LANES = 128
SSD_CHUNK = 128
DN_CHUNK = 64
NEUMANN_LEVELS = 6
NEG_BIG = -1e30
VMEM_LIMIT = 56 * 1024 * 1024

SM_DT = 0
SM_A = SSM_HEADS
SM_B = SSM_HEADS + DN_V_HEADS


def _sigmoid(x):
    return 0.5 * jnp.tanh(0.5 * x) + 0.5


def _silu(x):
    h = 0.5 * x
    return h + h * jnp.tanh(h)


def _softplus(x):
    return jnp.maximum(x, 0.0) + jnp.log(1.0 + jnp.exp(-jnp.abs(x)))


def _params(**kw):
    return pltpu.CompilerParams(vmem_limit_bytes=VMEM_LIMIT, **kw)


def _adaln_kernel(c_ref, w_ref, b_ref, o_ref):
    o_ref[...] = jnp.dot(c_ref[...], w_ref[...], preferred_element_type=F32,
                         precision=lax.Precision.HIGHEST) + b_ref[...]


def _adaln(c8, w_ada, b_ada):
    n = w_ada.shape[1]
    tn = 768
    return pl.pallas_call(
        _adaln_kernel,
        grid=(n // tn,),
        in_specs=[pl.BlockSpec((8, D_MODEL), lambda j: (0, 0)),
                  pl.BlockSpec((D_MODEL, tn), lambda j: (0, j)),
                  pl.BlockSpec((1, tn), lambda j: (0, j))],
        out_specs=pl.BlockSpec((8, tn), lambda j: (0, j)),
        out_shape=jax.ShapeDtypeStruct((8, n), F32),
        compiler_params=_params(dimension_semantics=("arbitrary",)),
        name="adaln",
    )(c8, w_ada, b_ada)


def _inproj_kernel(x_ref, sc_ref, sh_ref, w_ref, o_ref, h_ref):
    @pl.when(pl.program_id(1) == 0)
    def _():
        h_ref[...] = (x_ref[...] * (1.0 + sc_ref[...]) + sh_ref[...]).astype(BF16)

    o_ref[...] = jnp.dot(h_ref[...], w_ref[...], preferred_element_type=F32).astype(o_ref.dtype)


def _inproj(x2, scale, shift, w, out_dtype, tm, tn, name):
    s = x2.shape[0]
    n = w.shape[1]
    return pl.pallas_call(
        _inproj_kernel,
        grid=(s // tm, n // tn),
        in_specs=[pl.BlockSpec((tm, D_MODEL), lambda i, j: (i, 0)),
                  pl.BlockSpec((1, D_MODEL), lambda i, j: (0, 0)),
                  pl.BlockSpec((1, D_MODEL), lambda i, j: (0, 0)),
                  pl.BlockSpec((D_MODEL, tn), lambda i, j: (0, j))],
        out_specs=pl.BlockSpec((tm, tn), lambda i, j: (i, j)),
        out_shape=jax.ShapeDtypeStruct((s, n), out_dtype),
        scratch_shapes=[pltpu.VMEM((tm, D_MODEL), BF16)],
        compiler_params=_params(dimension_semantics=("arbitrary", "arbitrary")),
        name=name,
    )(x2, scale, shift, w)


CONV_T = 512
CONV_TC = 1024
CONV_RB = 128
CONV_HALO = 64
CONV_WIN = CONV_RB + 2 * CONV_HALO
CONV_PAD = (CONV_WIDTH - 1) // 2
CONV_SIDE_TAPS = tuple(t for t in range(CONV_WIDTH) if t != CONV_PAD)


def _conv_kernel(prev_ref, main_ref, next_ref, w_ref, b_ref, o_ref, ext_ref, *, normalize):
    i = pl.program_id(0)
    j = pl.program_id(1)
    prev = prev_ref[...]
    nxt = next_ref[...]
    ext_ref[0:CONV_HALO, :] = jnp.where(i > 0, prev, jnp.zeros_like(prev))
    ext_ref[CONV_HALO:CONV_HALO + CONV_T, :] = main_ref[...]
    ext_ref[CONV_HALO + CONV_T:, :] = jnp.where(i < pl.num_programs(0) - 1, nxt, jnp.zeros_like(nxt))
    w = w_ref[...]
    b = b_ref[...]

    n_side = len(CONV_SIDE_TAPS)
    ri = lax.broadcasted_iota(jnp.int32, (n_side * CONV_RB, CONV_WIN), 0)
    ci = lax.broadcasted_iota(jnp.int32, (n_side * CONV_RB, CONV_WIN), 1)
    grp = ri // CONV_RB
    tap = grp + (grp >= CONV_PAD).astype(jnp.int32)
    shift = (ci == (ri - grp * CONV_RB) + CONV_HALO + tap - CONV_PAD).astype(BF16)
    if normalize:
        qk_scale = jnp.where(j == 0, DN_HEAD ** -0.5, 1.0).astype(F32)

    for r in range(CONV_T // CONV_RB):
        r0 = r * CONV_RB
        win = ext_ref[r0:r0 + CONV_WIN, :]
        sh = jnp.dot(shift, win, preferred_element_type=F32)
        acc = b + w[CONV_PAD:CONV_PAD + 1, :] * win[CONV_HALO:CONV_HALO + CONV_RB, :].astype(F32)
        for g, t in enumerate(CONV_SIDE_TAPS):
            acc = acc + w[t:t + 1, :] * sh[g * CONV_RB:(g + 1) * CONV_RB, :]
        u = _silu(acc)
        if normalize:
            for hh in range(CONV_TC // DN_HEAD):
                uh = u[:, hh * DN_HEAD:(hh + 1) * DN_HEAD]
                inv = lax.rsqrt(jnp.sum(uh * uh, axis=-1, keepdims=True) + 1e-6) * qk_scale
                o_ref[r0:r0 + CONV_RB, hh * DN_HEAD:(hh + 1) * DN_HEAD] = (uh * inv).astype(o_ref.dtype)
        else:
            o_ref[r0:r0 + CONV_RB, :] = u.astype(o_ref.dtype)


def _conv(proj, conv_w, conv_b, first_block, n_blocks, normalize, name):
    s = proj.shape[0]
    nt = s // CONV_T
    hb = CONV_T // CONV_HALO
    last_halo = s // CONV_HALO - 1
    return pl.pallas_call(
        functools.partial(_conv_kernel, normalize=normalize),
        grid=(nt, n_blocks),
        in_specs=[pl.BlockSpec((CONV_HALO, CONV_TC), lambda i, j: (jnp.maximum(i * hb - 1, 0), j + first_block)),
                  pl.BlockSpec((CONV_T, CONV_TC), lambda i, j: (i, j + first_block)),
                  pl.BlockSpec((CONV_HALO, CONV_TC),
                               lambda i, j: (jnp.minimum((i + 1) * hb, last_halo), j + first_block)),
                  pl.BlockSpec((CONV_WIDTH, CONV_TC), lambda i, j: (0, j + first_block)),
                  pl.BlockSpec((1, CONV_TC), lambda i, j: (0, j + first_block))],
        out_specs=pl.BlockSpec((CONV_T, CONV_TC), lambda i, j: (i, j)),
        out_shape=jax.ShapeDtypeStruct((s, n_blocks * CONV_TC), BF16),
        scratch_shapes=[pltpu.VMEM((CONV_T + 2 * CONV_HALO, CONV_TC), BF16)],
        compiler_params=_params(dimension_semantics=("arbitrary", "arbitrary")),
        name=name,
    )(proj, proj, proj, conv_w, conv_b)


def _chunk_index(d, c, n):
    return c + d * (n - 1 - 2 * c)


def _tri(d, n):
    row = lax.broadcasted_iota(jnp.int32, (n, n), 0)
    col = lax.broadcasted_iota(jnp.int32, (n, n), 1)
    return (row - col) * (1 - 2 * d) >= 0


def _ssd_kernel(sm_ref, prm_ref, x_ref, b_ref, c_ref, y_ref, h_ref):
    d = pl.program_id(0)
    q = SSD_CHUNK

    @pl.when(pl.program_id(1) == 0)
    def _():
        h_ref[...] = jnp.zeros_like(h_ref)

    lane = lax.broadcasted_iota(jnp.int32, (1, LANES), 1)
    prm = prm_ref[...]
    a_row = jnp.where(lane < SSM_HEADS, -jnp.exp(prm[1:2, :]), 0.0)
    dt = _softplus(sm_ref[...] + prm[0:1, :])
    dta = dt * a_row
    mask = _tri(d, q)
    acum = jnp.dot(mask.astype(F32), dta, preferred_element_type=F32,
                   precision=lax.Precision.HIGHEST)
    total = jnp.sum(dta, axis=0, keepdims=True)
    acum_t = acum.T
    dt_t = dt.T

    erow = lax.broadcasted_iota(jnp.int32, (LANES, SSM_INNER), 0)
    ecol = lax.broadcasted_iota(jnp.int32, (LANES, SSM_INNER), 1)
    expand = (erow == ecol // SSM_HEAD_DIM).astype(BF16)

    def expand_heads(v):
        hi = v.astype(BF16)
        lo = (v - hi.astype(F32)).astype(BF16)
        return (jnp.dot(hi, expand, preferred_element_type=F32)
                + jnp.dot(lo, expand, preferred_element_type=F32))

    w_exp = expand_heads(dt * jnp.exp(total - acum))
    ea_exp = expand_heads(jnp.exp(acum))
    cd_exp = expand_heads(jnp.broadcast_to(jnp.exp(total), (8, LANES)))[0:1, :]

    x = x_ref[...]
    xw = (x.astype(F32) * w_exp).astype(BF16)
    glane = lax.broadcasted_iota(jnp.int32, (1, SSM_HPG * SSM_HEAD_DIM), 1) // SSM_HEAD_DIM

    for g in range(SSM_GROUPS):
        gs = slice(g * SSM_HPG * SSM_HEAD_DIM, (g + 1) * SSM_HPG * SSM_HEAD_DIM)
        ns = slice(g * SSM_STATE, (g + 1) * SSM_STATE)
        bg = b_ref[:, ns]
        cg = c_ref[:, ns]
        xg = x[:, gs]
        cb = lax.dot_general(cg, bg, (((1,), (1,)), ((), ())), preferred_element_type=F32)
        h_prev = h_ref[:, gs]
        y = jnp.dot(cg, h_prev.astype(BF16), preferred_element_type=F32) * ea_exp[:, gs]
        bg_t = bg.astype(F32).T.astype(BF16)
        st = jnp.dot(bg_t, xw[:, gs], preferred_element_type=F32)
        h_ref[:, gs] = h_prev * cd_exp[:, gs] + st
        for j in range(SSM_HPG):
            hd = g * SSM_HPG + j
            seg = acum[:, hd:hd + 1] - acum_t[hd:hd + 1, :]
            m = cb * jnp.exp(jnp.where(mask, seg, NEG_BIG)) * dt_t[hd:hd + 1, :]
            xm = jnp.where(glane == j, xg, jnp.zeros_like(xg))
            y = y + jnp.dot(m.astype(BF16), xm, preferred_element_type=F32)
        y_ref[:, gs] = y.astype(y_ref.dtype)


def _ssd(small, prm, u):
    s = u.shape[0]
    n = s // SSD_CHUNK
    q = SSD_CHUNK
    bc_block = SSM_INNER // (SSM_GROUPS * SSM_STATE)
    return pl.pallas_call(
        _ssd_kernel,
        grid=(2, n),
        in_specs=[pl.BlockSpec((q, LANES), lambda d, c: (_chunk_index(d, c, n), d)),
                  pl.BlockSpec((None, 8, LANES), lambda d, c: (d, 0, 0)),
                  pl.BlockSpec((q, SSM_INNER), lambda d, c: (_chunk_index(d, c, n), 0)),
                  pl.BlockSpec((q, SSM_GROUPS * SSM_STATE), lambda d, c: (_chunk_index(d, c, n), bc_block)),
                  pl.BlockSpec((q, SSM_GROUPS * SSM_STATE), lambda d, c: (_chunk_index(d, c, n), bc_block + 1))],
        out_specs=pl.BlockSpec((None, q, SSM_INNER), lambda d, c: (d, _chunk_index(d, c, n), 0)),
        out_shape=jax.ShapeDtypeStruct((2, s, SSM_INNER), F32),
        scratch_shapes=[pltpu.VMEM((SSM_STATE, SSM_INNER), F32)],
        compiler_params=_params(dimension_semantics=("arbitrary", "arbitrary")),
        name="ssd",
    )(small, prm, u, u, u)


def _dn_decay(d, sm_ref, prm_ref):
    c = DN_CHUNK
    lane = lax.broadcasted_iota(jnp.int32, (1, LANES), 1)
    prm = prm_ref[d]
    sm = sm_ref[...]
    a_row = jnp.where(jnp.logical_and(lane >= SM_A, lane < SM_B), -jnp.exp(prm[1:2, :]), 0.0)
    g = a_row * _softplus(sm + prm[0:1, :])
    gcum = jnp.dot(_tri(d, c).astype(F32), g, preferred_element_type=F32,
                   precision=lax.Precision.HIGHEST)
    row = lax.broadcasted_iota(jnp.int32, (c, LANES), 0)
    col = lax.broadcasted_iota(jnp.int32, (c, LANES), 1)
    ahead = (row - jnp.bitwise_and(col, c - 1)) * (1 - 2 * d)
    return dict(beta=_sigmoid(sm),
                gcum=gcum,
                glast=jnp.sum(g, axis=0, keepdims=True),
                gcum_t2=jnp.concatenate([gcum, gcum], axis=0).T,
                eg=jnp.exp(gcum),
                incl=ahead >= 0,
                strict=ahead > 0)


def _dn_kernel(smf_ref, smb_ref, prm_ref, qf_ref, kf_ref, vf_ref, qb_ref, kb_ref, vb_ref,
               of_ref, ob_ref, s_ref):
    c = DN_CHUNK

    @pl.when(pl.program_id(0) == 0)
    def _():
        s_ref[...] = jnp.zeros_like(s_ref)

    dec_d = [_dn_decay(0, smf_ref, prm_ref), _dn_decay(1, smb_ref, prm_ref)]
    q_refs, k_refs, v_refs, o_refs = (qf_ref, qb_ref), (kf_ref, kb_ref), (vf_ref, vb_ref), (of_ref, ob_ref)

    row = lax.broadcasted_iota(jnp.int32, (c, LANES), 0)
    col = lax.broadcasted_iota(jnp.int32, (c, LANES), 1)
    left = col < c
    eye_right = jnp.where(jnp.logical_and(row == col - c, jnp.logical_not(left)), 1.0, 0.0)
    eye_right16 = eye_right.astype(BF16)
    rep = DN_V_HEADS // DN_K_HEADS
    nt = (((1,), (1,)), ((), ()))
    chains = [(d, hv) for d in range(2) for hv in range(DN_V_HEADS)]

    kk2, qk2, kf, qf, k_t = {}, {}, {}, {}, {}
    for d in range(2):
        for kh in range(DN_K_HEADS):
            ks = slice(kh * DN_HEAD, (kh + 1) * DN_HEAD)
            qh = q_refs[d][:, ks]
            kin = k_refs[d][:, ks]
            k2 = jnp.concatenate([kin, kin], axis=0)
            kk2[d, kh] = lax.dot_general(kin, k2, nt, preferred_element_type=F32)
            qk2[d, kh] = lax.dot_general(qh, k2, nt, preferred_element_type=F32)
            kf[d, kh] = kin.astype(F32)
            qf[d, kh] = qh.astype(F32)
            k_t[d, kh] = jnp.concatenate([kf[d, kh], kf[d, kh]], axis=0).T[:, :c]

    gc_row, eg_col, gl, attn, rhs_pad, rhs, tpad = {}, {}, {}, {}, {}, {}, {}
    for ch in chains:
        d, hv = ch
        dd = dec_d[d]
        kh = hv // rep
        la = SM_A + hv
        lb = SM_B + hv
        gc_col = dd["gcum"][:, la:la + 1]
        gc_row[ch] = dd["gcum_t2"][la:la + 1, :]
        beta_col = dd["beta"][:, lb:lb + 1]
        eg_col[ch] = dd["eg"][:, la:la + 1]
        gl[ch] = dd["glast"][:, la:la + 1]
        dec = jnp.exp(jnp.where(dd["incl"], gc_col - gc_row[ch], NEG_BIG))
        x2 = jnp.where(dd["strict"], kk2[d, kh] * (-beta_col) * dec, 0.0)
        attn[ch] = (qk2[d, kh] * dec).astype(BF16)
        vb = v_refs[d][:, hv * DN_HEAD:(hv + 1) * DN_HEAD].astype(F32) * beta_col
        kb = kf[d, kh] * (beta_col * eg_col[ch])
        rhs_in = jnp.concatenate([vb, kb], axis=1).astype(BF16)
        rhs_pad[ch] = jnp.concatenate([jnp.zeros_like(rhs_in), rhs_in], axis=0)
        rhs[ch] = jnp.where(left, x2, eye_right).astype(BF16)
        tpad[ch] = eye_right16

    for _ in range(NEUMANN_LEVELS):
        for ch in chains:
            res = jnp.dot(rhs[ch][:, :c], rhs[ch], preferred_element_type=F32).astype(BF16)
            rhs[ch] = res + tpad[ch]
            tpad[ch] = jnp.where(left, jnp.zeros_like(res), rhs[ch])

    uw = {ch: jnp.dot(tpad[ch], rhs_pad[ch], preferred_element_type=F32) for ch in chains}
    state = {ch: s_ref[ch[0], ch[1]] for ch in chains}
    ws_qs = {}
    for ch in chains:
        d, hv = ch
        wq = jnp.concatenate([uw[ch][:, DN_HEAD:], qf[d, hv // rep] * eg_col[ch]], axis=0).astype(BF16)
        ws_qs[ch] = jnp.dot(wq, state[ch].astype(BF16), preferred_element_type=F32)
    v_new = {ch: (uw[ch][:, :DN_HEAD] - ws_qs[ch][:c]).astype(BF16) for ch in chains}
    for ch in chains:
        d, hv = ch
        o = ws_qs[ch][c:] + jnp.dot(attn[ch][:, :c], v_new[ch], preferred_element_type=F32)
        o_refs[d][:, hv * DN_HEAD:(hv + 1) * DN_HEAD] = o.astype(o_refs[d].dtype)
    for ch in chains:
        d, hv = ch
        kd_t = (k_t[d, hv // rep] * jnp.exp(gl[ch] - gc_row[ch][:, :c])).astype(BF16)
        s_ref[d, hv] = state[ch] * jnp.exp(gl[ch]) + jnp.dot(kd_t, v_new[ch], preferred_element_type=F32)


def _dn(small, prm, qk, v):
    s = v.shape[0]
    n = s // DN_CHUNK
    c = DN_CHUNK
    fwd = lambda i: i
    bwd = lambda i: n - 1 - i

    def specs(chunk):
        return [pl.BlockSpec((c, DN_KEY_DIM), lambda i: (chunk(i), 0)),
                pl.BlockSpec((c, DN_KEY_DIM), lambda i: (chunk(i), 1)),
                pl.BlockSpec((c, DN_VAL_DIM), lambda i: (chunk(i), 0))]

    return pl.pallas_call(
        _dn_kernel,
        grid=(n,),
        in_specs=[pl.BlockSpec((c, LANES), lambda i: (fwd(i), 0)),
                  pl.BlockSpec((c, LANES), lambda i: (bwd(i), 1)),
                  pl.BlockSpec((2, 8, LANES), lambda i: (0, 0, 0))] + specs(fwd) + specs(bwd),
        out_specs=[pl.BlockSpec((c, DN_VAL_DIM), lambda i: (fwd(i), 0)),
                   pl.BlockSpec((c, DN_VAL_DIM), lambda i: (bwd(i), 0))],
        out_shape=[jax.ShapeDtypeStruct((s, DN_VAL_DIM), F32)] * 2,
        scratch_shapes=[pltpu.VMEM((2, DN_V_HEADS, DN_HEAD, DN_HEAD), F32)],
        compiler_params=_params(dimension_semantics=("arbitrary",)),
        name="dn",
    )(small, small, prm, qk, qk, v, qk, qk, v)


OUT_TM = 256
Z_SSM_BLOCK = CONV_DIM // D_MODEL
Z_DN_BLOCK = Z_SSM_BLOCK + 1
G_SSM_BLOCK = Z_SSM_BLOCK + 2
G_DN_BLOCK = Z_SSM_BLOCK + 3


def _ssm_out_kernel(yf_ref, yb_ref, x_ref, z_ref, dsk_ref, nw_ref, w_ref, o_ref):
    z = z_ref[...].astype(F32)
    y = (yf_ref[...] + yb_ref[...] + dsk_ref[...] * x_ref[...].astype(F32)) * _silu(z)
    y = y * lax.rsqrt(jnp.mean(y * y, axis=-1, keepdims=True) + RMS_EPS) * nw_ref[...]
    o_ref[...] = jnp.dot(y.astype(BF16), w_ref[...], preferred_element_type=F32).astype(o_ref.dtype)


def _dn_out_kernel(of_ref, ob_ref, z_ref, nw_ref, w_ref, o_ref, y_ref):
    nw = nw_ref[...]
    for h in range(DN_V_HEADS):
        hs = slice(h * DN_HEAD, (h + 1) * DN_HEAD)
        o = of_ref[:, hs] + ob_ref[:, hs]
        z = z_ref[:, hs].astype(F32)
        o = o * lax.rsqrt(jnp.mean(o * o, axis=-1, keepdims=True) + RMS_EPS) * nw
        y_ref[:, hs] = (o * _silu(z)).astype(BF16)
    o_ref[...] = jnp.dot(y_ref[...], w_ref[...], preferred_element_type=F32).astype(o_ref.dtype)


def _merge_kernel(ps_ref, pd_ref, gs_ref, gd_ref, w_ref, x_ref, gate_ref, lg_ref, lb_ref, o_ref):
    mix = (_sigmoid(gs_ref[...].astype(F32)) * ps_ref[...].astype(F32)
           + _sigmoid(gd_ref[...].astype(F32)) * pd_ref[...].astype(F32))
    mixed = jnp.dot(mix.astype(BF16), w_ref[...], preferred_element_type=F32)
    r = DEEPNORM_ALPHA * x_ref[...] + gate_ref[...] * mixed
    mu = jnp.mean(r, axis=-1, keepdims=True)
    rc = r - mu
    var = jnp.mean(rc * rc, axis=-1, keepdims=True)
    o_ref[...] = rc * lax.rsqrt(var + LN_EPS) * lg_ref[...] + lb_ref[...]


def _row_spec(tm, width, block_col=0):
    return pl.BlockSpec((tm, width), lambda i: (i, block_col))


def _const_spec(shape):
    return pl.BlockSpec(shape, lambda i: (0,) * len(shape))


def _dir_spec(tm, width, d):
    return pl.BlockSpec((None, tm, width), lambda i: (d, i, 0))


def _ssm_out(y, u, proj, d_skip, norm_w, w):
    s = u.shape[0]
    tm = OUT_TM
    return pl.pallas_call(
        _ssm_out_kernel,
        grid=(s // tm,),
        in_specs=[_dir_spec(tm, SSM_INNER, 0), _dir_spec(tm, SSM_INNER, 1),
                  _row_spec(tm, SSM_INNER, 0), _row_spec(tm, SSM_INNER, Z_SSM_BLOCK),
                  _const_spec((1, SSM_INNER)), _const_spec((1, SSM_INNER)),
                  _const_spec((SSM_INNER, D_MODEL))],
        out_specs=_row_spec(tm, D_MODEL),
        out_shape=jax.ShapeDtypeStruct((s, D_MODEL), BF16),
        compiler_params=_params(dimension_semantics=("arbitrary",)),
        name="ssm_out",
    )(y, y, u, proj, d_skip, norm_w, w)


def _dn_out(o_f, o_b, proj, norm_w, w):
    s = proj.shape[0]
    tm = OUT_TM
    return pl.pallas_call(
        _dn_out_kernel,
        grid=(s // tm,),
        in_specs=[_row_spec(tm, DN_VAL_DIM), _row_spec(tm, DN_VAL_DIM),
                  _row_spec(tm, DN_VAL_DIM, Z_DN_BLOCK),
                  _const_spec((1, DN_HEAD)), _const_spec((DN_VAL_DIM, D_MODEL))],
        out_specs=_row_spec(tm, D_MODEL),
        out_shape=jax.ShapeDtypeStruct((s, D_MODEL), BF16),
        scratch_shapes=[pltpu.VMEM((tm, DN_VAL_DIM), BF16)],
        compiler_params=_params(dimension_semantics=("arbitrary",)),
        name="dn_out",
    )(o_f, o_b, proj, norm_w, w)


def _merge(p_ssm, p_dn, proj, w_out, x2, gate, ln_g, ln_b):
    s = x2.shape[0]
    tm = OUT_TM
    return pl.pallas_call(
        _merge_kernel,
        grid=(s // tm,),
        in_specs=[_row_spec(tm, D_MODEL), _row_spec(tm, D_MODEL),
                  _row_spec(tm, D_MODEL, G_SSM_BLOCK), _row_spec(tm, D_MODEL, G_DN_BLOCK),
                  _const_spec((D_MODEL, D_MODEL)), _row_spec(tm, D_MODEL),
                  _const_spec((1, D_MODEL)), _const_spec((1, D_MODEL)), _const_spec((1, D_MODEL))],
        out_specs=_row_spec(tm, D_MODEL),
        out_shape=jax.ShapeDtypeStruct((s, D_MODEL), F32),
        compiler_params=_params(dimension_semantics=("arbitrary",)),
        name="merge",
    )(p_ssm, p_dn, proj, proj, w_out, x2, gate, ln_g, ln_b)


def _small_weight(w_in):
    base = CONV_DIM + SSM_INNER + DN_VAL_DIM
    dt0 = base
    a0 = dt0 + 2 * SSM_HEADS
    b0 = a0 + 2 * DN_V_HEADS
    tiles = []
    for d in range(2):
        tiles += [w_in[:, dt0 + d * SSM_HEADS:dt0 + (d + 1) * SSM_HEADS],
                  w_in[:, a0 + d * DN_V_HEADS:a0 + (d + 1) * DN_V_HEADS],
                  w_in[:, b0 + d * DN_V_HEADS:b0 + (d + 1) * DN_V_HEADS],
                  jnp.zeros((D_MODEL, LANES - SSM_HEADS - 2 * DN_V_HEADS), w_in.dtype)]
    return jnp.concatenate(tiles, axis=1)


def _dir_params(ssm_dt_bias, dn_dt_bias, ssm_a_log, dn_a_log):
    pad = jnp.zeros((2, LANES - SSM_HEADS - DN_V_HEADS), F32)
    bias = jnp.concatenate([ssm_dt_bias, dn_dt_bias, pad], axis=1)
    alog = jnp.concatenate([ssm_a_log, dn_a_log, pad], axis=1)
    rest = jnp.zeros((2, 6, LANES), F32)
    return jnp.concatenate([bias[:, None, :], alog[:, None, :], rest], axis=1)


def kernel(x, c, w_ada, b_ada, w_in, conv_w, conv_b, ssm_a_log, ssm_dt_bias, ssm_d, ssm_norm_w,
           dn_a_log, dn_dt_bias, dn_norm_w, w_branch_ssm, w_branch_dn, w_out, ln_g, ln_b):
    bsz, s, dm = x.shape
    assert bsz == 1 and dm == D_MODEL and w_ada.shape[0] == 1
    x2 = x[0]

    mod = _adaln(jnp.broadcast_to(c, (8, D_MODEL)), w_ada[0], b_ada)
    shift = mod[0:1, 0:D_MODEL]
    scale = mod[0:1, D_MODEL:2 * D_MODEL]
    gate = mod[0:1, 2 * D_MODEL:3 * D_MODEL]

    n_main = CONV_DIM + SSM_INNER + DN_VAL_DIM
    w16 = w_in[0].astype(BF16)
    w_main = jnp.concatenate([w16[:, :n_main], w16[:, n_main + N_SMALL:]], axis=1)
    w_small = _small_weight(w16)
    proj = _inproj(x2, scale, shift, w_main, BF16, 1024, 1024, "inproj")
    small = _inproj(x2, scale, shift, w_small, F32, 1024, 2 * LANES, "inproj_small")

    ssd_blocks = (SSM_INNER + 2 * SSM_GROUPS * SSM_STATE) // CONV_TC
    qk_blocks = 2 * DN_KEY_DIM // CONV_TC
    u = _conv(proj, conv_w[0], conv_b, 0, ssd_blocks, False, "conv_ssd")
    qk = _conv(proj, conv_w[0], conv_b, ssd_blocks, qk_blocks, True, "conv_qk")
    v = _conv(proj, conv_w[0], conv_b, ssd_blocks + qk_blocks, DN_VAL_DIM // CONV_TC, False, "conv_v")
    prm = _dir_params(ssm_dt_bias[0], dn_dt_bias[0], ssm_a_log[0], dn_a_log[0])
    y = _ssd(small, prm, u)
    o_f, o_b = _dn(small, prm, qk, v)

    d_skip = jnp.repeat(ssm_d[0], SSM_HEAD_DIM)[None, :]
    p_ssm = _ssm_out(y, u, proj, d_skip, ssm_norm_w, w_branch_ssm[0].astype(BF16))
    p_dn = _dn_out(o_f, o_b, proj, dn_norm_w, w_branch_dn[0].astype(BF16))
    out = _merge(p_ssm, p_dn, proj, w_out[0].astype(BF16), x2, gate, ln_g, ln_b)
    return out[None]
```
